```python
import jax, jax.numpy as jnp
from jax import lax
import numpy as np

D_MODEL = 2048
BATCH = 16
SEQ = 2048
DEPTH = 4

N_META = 16
POOL_WINDOWS = (2, 4, 8, 16)
N_POOL_GROUPS = len(POOL_WINDOWS)
POOL_GROUP = D_MODEL // N_POOL_GROUPS
HEAD_DIM = 64
N_Q_HEADS = D_MODEL // HEAD_DIM
N_KV_HEADS = N_Q_HEADS // 8
Q_PER_KV = N_Q_HEADS // N_KV_HEADS
WINDOW = 128
BLOCK = 128
ROT_DIM = HEAD_DIM // 4
ROPE_THETA = 500000.0
D_FF = 7 * D_MODEL // 2
N_EXPERTS = 8
TOP_K = 2
NORM_EPS = 1e-5
N_A = DEPTH // 2
N_B = DEPTH - N_A
N_DENSE = (DEPTH + 1) // 2
N_MOE = DEPTH // 2

kernel_name = "yoco_pool_swa_sink_moe_hybrid"


def rms_norm(x, g):
    xf = x.astype(jnp.float32)
    y = xf * lax.rsqrt(jnp.mean(xf * xf, axis=-1, keepdims=True) + NORM_EPS)
    return (y * g.astype(jnp.float32)).astype(x.dtype)


def rope_partial(x, pos):
    half = ROT_DIM // 2
    inv = jnp.power(jnp.float32(ROPE_THETA), -jnp.arange(half, dtype=jnp.float32) / half)
    ang = pos.astype(jnp.float32)[:, None] * inv[None, :]
    cos = jnp.cos(ang)[None, :, None, :]
    sin = jnp.sin(ang)[None, :, None, :]
    xr = x[..., :ROT_DIM].astype(jnp.float32)
    x1, x2 = xr[..., :half], xr[..., half:]
    rot = jnp.concatenate([x1 * cos - x2 * sin, x2 * cos + x1 * sin], axis=-1).astype(x.dtype)
    return jnp.concatenate([rot, x[..., ROT_DIM:]], axis=-1)


def causal_multiscale_pool(h):
    B, L, D = h.shape
    hg = h.reshape(B, L, N_POOL_GROUPS, POOL_GROUP).astype(jnp.float32)
    c = jnp.cumsum(hg, axis=1)
    c = jnp.concatenate([jnp.zeros_like(c[:, :1]), c], axis=1)
    t = jnp.arange(L)
    outs = []
    for g, w in enumerate(POOL_WINDOWS):
        start = jnp.maximum(t + 1 - w, 0)
        cg = c[:, :, g]
        s = cg[:, t + 1] - cg[:, start]
        cnt = (t + 1 - start).astype(jnp.float32)
        outs.append(s / cnt[None, :, None])
    pooled = jnp.stack(outs, axis=2)
    return (pooled - hg).astype(h.dtype)


def sink_attend(q, k, v, mask, sinks):
    s = jnp.einsum('...qhgd,...khd->...hgqk', q, k).astype(jnp.float32) * (HEAD_DIM ** -0.5)
    s = jnp.where(mask, s, -jnp.inf)
    sink = sinks.astype(jnp.float32)[:, :, None, None]
    m = jnp.maximum(jnp.max(s, axis=-1, keepdims=True), sink)
    p = jnp.exp(s - m)
    denom = jnp.sum(p, axis=-1, keepdims=True) + jnp.exp(sink - m)
    p = (p / denom).astype(v.dtype)
    return jnp.einsum('...hgqk,...khd->...qhgd', p, v)


def swa_with_meta(q, k, v, sinks):
    B, L = q.shape[0], q.shape[1]
    S = L - N_META
    nb = S // BLOCK
    qm, qr = q[:, :N_META], q[:, N_META:]
    km, kr = k[:, :N_META], k[:, N_META:]
    vm, vr = v[:, :N_META], v[:, N_META:]
    meta_mask = jnp.tril(jnp.ones((N_META, N_META), dtype=bool))
    out_meta = sink_attend(qm, km, vm, meta_mask, sinks)

    def blockify(t):
        return jnp.moveaxis(t.reshape((B, nb, BLOCK) + t.shape[2:]), 1, 0)

    qb, kb, vb = blockify(qr), blockify(kr), blockify(vr)
    kp = jnp.concatenate([jnp.zeros_like(kb[:1]), kb[:-1]], axis=0)
    vp = jnp.concatenate([jnp.zeros_like(vb[:1]), vb[:-1]], axis=0)
    i = jnp.arange(BLOCK)
    cur_band = (i[:, None] >= i[None, :]) & ((i[:, None] - i[None, :]) < WINDOW)
    prev_band = (BLOCK + i[:, None] - i[None, :]) < WINDOW
    meta_vis = jnp.ones((BLOCK, N_META), dtype=bool)

    def one_block(args):
        n, qn, kpn, kn, vpn, vn = args
        keys = jnp.concatenate([km, kpn, kn], axis=1)
        vals = jnp.concatenate([vm, vpn, vn], axis=1)
        mask = jnp.concatenate([meta_vis, prev_band & (n > 0), cur_band], axis=1)
        return sink_attend(qn, keys, vals, mask, sinks)

    out = lax.map(one_block, (jnp.arange(nb), qb, kp, kb, vp, vb))
    out = jnp.moveaxis(out, 0, 1).reshape((B, S) + q.shape[2:])
    return jnp.concatenate([out_meta, out], axis=1)


def swiglu(h, wg, wu, wd):
    return (jax.nn.silu(h @ wg) * (h @ wu)) @ wd


def moe_swiglu(h, w_router, wg, wu, wd):
    B, L, D = h.shape
    ht = h.reshape(-1, D)
    logits = (ht @ w_router).astype(jnp.float32)
    top_val, top_idx = lax.top_k(logits, TOP_K)
    gates = jax.nn.softmax(top_val, axis=-1)
    combine = jnp.sum(jax.nn.one_hot(top_idx, N_EXPERTS, dtype=jnp.float32) * gates[..., None], axis=1)
    y = jnp.zeros(ht.shape, jnp.float32)
    for e in range(N_EXPERTS):
        y = y + combine[:, e:e + 1] * swiglu(ht, wg[e], wu[e], wd[e]).astype(jnp.float32)
    return y.astype(h.dtype).reshape(B, L, D)


def setup_inputs(seed: int = 0) -> dict:
    key = jax.random.key(seed)
    ks = jax.random.split(key, 24)
    D = D_MODEL
    f32 = jnp.float32

    def nrm(k, shape, fan_in):
        return jax.random.normal(k, shape, f32) * (fan_in ** -0.5)

    return {
        "x": jax.random.normal(ks[0], (BATCH, SEQ, D), f32),
        "meta_tokens": jax.random.normal(ks[1], (N_META, D), f32),
        "norm_mix": 1.0 + 0.02 * jax.random.normal(ks[2], (DEPTH, D), f32),
        "norm_ffn": 1.0 + 0.02 * jax.random.normal(ks[3], (DEPTH, D), f32),
        "norm_kv": 1.0 + 0.02 * jax.random.normal(ks[4], (D,), f32),
        "norm_final": 1.0 + 0.02 * jax.random.normal(ks[5], (D,), f32),
        "pool_w": nrm(ks[6], (N_A, N_POOL_GROUPS, POOL_GROUP, POOL_GROUP), POOL_GROUP),
        "pool_scale": 1.0 + 0.1 * jax.random.normal(ks[7], (N_A, D), f32),
        "w_kv": nrm(ks[8], (D, 2 * N_KV_HEADS * HEAD_DIM), D),
        "w_q": nrm(ks[9], (N_B, D, N_Q_HEADS * HEAD_DIM), D),
        "sinks": jax.random.normal(ks[10], (N_B, N_Q_HEADS), f32),
        "w_o": nrm(ks[11], (N_B, N_Q_HEADS * HEAD_DIM, D), N_Q_HEADS * HEAD_DIM),
        "dense_w_gate": nrm(ks[12], (N_DENSE, D, D_FF), D),
        "dense_w_up": nrm(ks[13], (N_DENSE, D, D_FF), D),
        "dense_w_down": nrm(ks[14], (N_DENSE, D_FF, D), D_FF),
        "router_w": nrm(ks[15], (N_MOE, D, N_EXPERTS), D),
        "moe_w_gate": nrm(ks[16], (N_MOE, N_EXPERTS, D, D_FF), D),
        "moe_w_up": nrm(ks[17], (N_MOE, N_EXPERTS, D, D_FF), D),
        "moe_w_down": nrm(ks[18], (N_MOE, N_EXPERTS, D_FF, D), D_FF),
    }


def reference(x, meta_tokens, norm_mix, norm_ffn, norm_kv, norm_final, pool_w, pool_scale,
              w_kv, w_q, sinks, w_o, dense_w_gate, dense_w_up, dense_w_down,
              router_w, moe_w_gate, moe_w_up, moe_w_down):
    B, S, D = x.shape
    L = N_META + S
    h = jnp.concatenate([jnp.broadcast_to(meta_tokens[None].astype(x.dtype), (B, N_META, D)), x], axis=1)
    pos = jnp.arange(L)
    k_shared = None
    v_shared = None
    for layer in range(DEPTH):
        u = rms_norm(h, norm_mix[layer])
        if layer < N_A:
            p = causal_multiscale_pool(u)
            y = jnp.einsum('blgc,gcd->blgd', p, pool_w[layer]).reshape(B, L, D)
            h = h + y * pool_scale[layer]
        else:
            b = layer - N_A
            q = (u @ w_q[b]).reshape(B, L, N_Q_HEADS, HEAD_DIM)
            q = rope_partial(q, pos).reshape(B, L, N_KV_HEADS, Q_PER_KV, HEAD_DIM)
            o = swa_with_meta(q, k_shared, v_shared, sinks[b].reshape(N_KV_HEADS, Q_PER_KV))
            h = h + o.reshape(B, L, N_Q_HEADS * HEAD_DIM) @ w_o[b]
        u = rms_norm(h, norm_ffn[layer])
        j = layer // 2
        if layer % 2 == 0:
            h = h + swiglu(u, dense_w_gate[j], dense_w_up[j], dense_w_down[j])
        else:
            h = h + moe_swiglu(u, router_w[j], moe_w_gate[j], moe_w_up[j], moe_w_down[j])
        if layer == N_A - 1:
            kv_in = rms_norm(h, norm_kv)
            kv = (kv_in @ w_kv).reshape(B, L, 2, N_KV_HEADS, HEAD_DIM)
            k_shared = rope_partial(kv[:, :, 0], pos)
            v_shared = kv[:, :, 1]
    return rms_norm(h, norm_final)[:, N_META:]
```

```python
import functools

import jax
import jax.numpy as jnp
from jax import lax
from jax.experimental import pallas as pl
from jax.experimental.pallas import tpu as pltpu

HEAD_DIM = 64
Q_PER_KV = 8
WINDOW = 128
ROT_DIM = 16
ROPE_THETA = 500000.0
NORM_EPS = 1e-5
POOL_WINDOWS = (2, 4, 8, 16)
TOP_K = 2
N_META = 16
PAD_ROWS = 128
LANES = 128
VMEM_LIMIT = 56 * 1024 * 1024

F32 = jnp.float32
BF16 = jnp.bfloat16


def _cdiv(a, b):
    return (a + b - 1) // b


def _params(sem):
    return pltpu.CompilerParams(dimension_semantics=sem, vmem_limit_bytes=VMEM_LIMIT)


def _rms(x, g):
    return x * lax.rsqrt(jnp.mean(x * x, axis=-1, keepdims=True) + NORM_EPS) * g


def _pick(n, prefs):
    for p in prefs:
        if n % p == 0:
            return p
    return n


def _pool_kernel(h_ref, halo_ref, g_ref, w_ref, sc_ref, o_ref, ext_ref, *, meta_blk, tl, halo):
    is_meta = pl.program_id(0) == meta_blk
    g = g_ref[...]
    u = _rms(h_ref[...], g)
    uh = _rms(halo_ref[...], g)
    ext_ref[0:halo, :] = jnp.where(is_meta, 0.0, uh)
    ext_ref[halo:, :] = u
    row = lax.broadcasted_iota(jnp.int32, (tl, 1), 0)
    c = h_ref.shape[1] // len(POOL_WINDOWS)
    for gi, w in enumerate(POOL_WINDOWS):
        cols = slice(gi * c, (gi + 1) * c)
        s = ext_ref[halo:halo + tl, cols]
        for k in range(1, w):
            s = s + ext_ref[halo - k:halo - k + tl, cols]
        cnt = jnp.where(is_meta, jnp.minimum(row + 1, w), w).astype(F32)
        p = s / cnt - ext_ref[halo:halo + tl, cols]
        y = jnp.dot(p.astype(BF16), w_ref[gi], preferred_element_type=F32)
        o_ref[:, cols] = h_ref[:, cols] + y * sc_ref[:, cols]


def _pool_layer(h, g, w, sc, *, n_real, seq):
    t, d = h.shape
    halo = max(POOL_WINDOWS)
    tl = _pick(seq, (256, 128))
    bps = seq // tl
    meta_blk = n_real // tl
    meta_halo = n_real // halo
    nb = _cdiv(t, tl)

    def halo_map(i):
        return (jnp.where(i % bps == 0, meta_halo, i * (tl // halo) - 1), 0)

    return pl.pallas_call(
        functools.partial(_pool_kernel, meta_blk=meta_blk, tl=tl, halo=halo),
        grid=(nb,),
        in_specs=[
            pl.BlockSpec((tl, d), lambda i: (i, 0)),
            pl.BlockSpec((halo, d), halo_map),
            pl.BlockSpec((1, d), lambda i: (0, 0)),
            pl.BlockSpec(w.shape, lambda i: (0, 0, 0)),
            pl.BlockSpec((1, d), lambda i: (0, 0)),
        ],
        out_specs=pl.BlockSpec((tl, d), lambda i: (i, 0)),
        out_shape=jax.ShapeDtypeStruct((t, d), F32),
        scratch_shapes=[pltpu.VMEM((halo + tl, d), F32)],
        compiler_params=_params(("parallel",)),
        name="pool_layer",
    )(h, h, g, w, sc)


def _swiglu_step(xn, wg, wu, wd):
    a = jnp.dot(xn, wg, preferred_element_type=F32)
    b = jnp.dot(xn, wu, preferred_element_type=F32)
    act = (a * jax.nn.sigmoid(a) * b).astype(BF16)
    return jnp.dot(act, wd, preferred_element_type=F32)


def _ffn_kernel(h_ref, g_ref, wg_ref, wu_ref, wd_ref, o_ref, xn_ref):
    j = pl.program_id(1)

    @pl.when(j == 0)
    def _():
        xn_ref[...] = _rms(h_ref[...], g_ref[...]).astype(BF16)

    y = _swiglu_step(xn_ref[...], wg_ref[...], wu_ref[...], wd_ref[...])

    @pl.when(j == 0)
    def _():
        o_ref[...] = h_ref[...] + y

    @pl.when(j > 0)
    def _():
        o_ref[...] += y


def _ffn_tiles(t, f):
    return (512 if t >= 4096 else 128), _pick(f, (512, 256, 128))


def _ffn_dense(h, g, wg, wu, wd):
    t, d = h.shape
    f = wg.shape[1]
    tm, tf = _ffn_tiles(t, f)
    return pl.pallas_call(
        _ffn_kernel,
        grid=(_cdiv(t, tm), f // tf),
        in_specs=[
            pl.BlockSpec((tm, d), lambda i, j: (i, 0)),
            pl.BlockSpec((1, d), lambda i, j: (0, 0)),
            pl.BlockSpec((d, tf), lambda i, j: (0, j)),
            pl.BlockSpec((d, tf), lambda i, j: (0, j)),
            pl.BlockSpec((tf, d), lambda i, j: (j, 0)),
        ],
        out_specs=pl.BlockSpec((tm, d), lambda i, j: (i, 0)),
        out_shape=jax.ShapeDtypeStruct((t, d), F32),
        scratch_shapes=[pltpu.VMEM((tm, d), BF16)],
        compiler_params=_params(("parallel", "arbitrary")),
        name="ffn_dense",
    )(h, g, wg, wu, wd)


def _router_kernel(h_ref, g_ref, w_ref, idx_ref, gate_ref, *, n_exp):
    xn = _rms(h_ref[...], g_ref[...])
    logits = jnp.dot(xn, w_ref[...], preferred_element_type=F32,
                     precision=lax.Precision.HIGHEST)
    lane = lax.broadcasted_iota(jnp.int32, logits.shape, 1).astype(F32)
    neg = jnp.float32(-jnp.inf)
    l1 = jnp.where(lane < n_exp, logits, neg)
    m1 = jnp.max(l1, axis=-1, keepdims=True)
    i1 = jnp.min(jnp.where(l1 == m1, lane, float(LANES)), axis=-1, keepdims=True)
    l2 = jnp.where(lane == i1, neg, l1)
    m2 = jnp.max(l2, axis=-1, keepdims=True)
    i2 = jnp.min(jnp.where(l2 == m2, lane, float(LANES)), axis=-1, keepdims=True)
    e2 = jnp.exp(m2 - m1)
    den = 1.0 + e2
    col = lax.broadcasted_iota(jnp.int32, idx_ref.shape, 1)
    idx_ref[...] = jnp.where(col == 0, i1, i2).astype(jnp.int32)
    gate_ref[...] = jnp.where(col == 0, 1.0 / den, e2 / den)


def _router(h, g, w_pad, n_exp):
    t, d = h.shape
    tm = min(256, t)
    return pl.pallas_call(
        functools.partial(_router_kernel, n_exp=n_exp),
        grid=(_cdiv(t, tm),),
        in_specs=[
            pl.BlockSpec((tm, d), lambda i: (i, 0)),
            pl.BlockSpec((1, d), lambda i: (0, 0)),
            pl.BlockSpec(w_pad.shape, lambda i: (0, 0)),
        ],
        out_specs=[pl.BlockSpec((tm, TOP_K), lambda i: (i, 0)),
                   pl.BlockSpec((tm, TOP_K), lambda i: (i, 0))],
        out_shape=[jax.ShapeDtypeStruct((t, TOP_K), jnp.int32),
                   jax.ShapeDtypeStruct((t, TOP_K), F32)],
        compiler_params=_params(("parallel",)),
        name="router",
    )(h, g, w_pad)


def _moe_kernel(te_ref, tv_ref, tok_ref, gate_ref, h_hbm, g_ref, wg_ref, wu_ref, wd_ref,
                o_ref, xbuf, xn_ref, sem, *, tm):
    i = pl.program_id(0)
    j = pl.program_id(1)
    valid = tv_ref[i] > 0

    def row_copy(r):
        return pltpu.make_async_copy(h_hbm.at[pl.ds(tok_ref[0, 0, r], 1)],
                                     xbuf.at[pl.ds(r, 1)], sem)

    @pl.when(jnp.logical_and(valid, j == 0))
    def _():
        def issue(r, c):
            row_copy(r).start()
            return c

        lax.fori_loop(0, tm, issue, 0)

        def drain(r, c):
            row_copy(r).wait()
            return c

        lax.fori_loop(0, tm, drain, 0)
        xn_ref[...] = _rms(xbuf[...], g_ref[...]).astype(BF16)

    @pl.when(jnp.logical_and(jnp.logical_not(valid), j == 0))
    def _():
        o_ref[...] = jnp.zeros_like(o_ref)

    @pl.when(valid)
    def _():
        y = _swiglu_step(xn_ref[...], wg_ref[0], wu_ref[0], wd_ref[0]) * gate_ref[...]

        @pl.when(j == 0)
        def _():
            o_ref[...] = y

        @pl.when(j > 0)
        def _():
            o_ref[...] += y


def _moe_ffn(h, g, wg, wu, wd, tile_exp, tile_valid, row_tok, row_gate, *, tm, tf):
    d = h.shape[1]
    f = wg.shape[2]
    nt = tile_exp.shape[0]
    nj = f // tf

    def wcol(i, j, te, tv):
        return (te[i], 0, jnp.where(tv[i] > 0, j, nj - 1))

    def wrow(i, j, te, tv):
        return (te[i], jnp.where(tv[i] > 0, j, nj - 1), 0)

    grid_spec = pltpu.PrefetchScalarGridSpec(
        num_scalar_prefetch=2,
        grid=(nt, nj),
        in_specs=[
            pl.BlockSpec((1, 1, tm), lambda i, j, te, tv: (i, 0, 0), memory_space=pltpu.SMEM),
            pl.BlockSpec((tm, 1), lambda i, j, te, tv: (i, 0)),
            pl.BlockSpec(memory_space=pl.ANY),
            pl.BlockSpec((1, d), lambda i, j, te, tv: (0, 0)),
            pl.BlockSpec((1, d, tf), wcol),
            pl.BlockSpec((1, d, tf), wcol),
            pl.BlockSpec((1, tf, d), wrow),
        ],
        out_specs=pl.BlockSpec((tm, d), lambda i, j, te, tv: (i, 0)),
        scratch_shapes=[pltpu.VMEM((tm, d), F32), pltpu.VMEM((tm, d), BF16),
                        pltpu.SemaphoreType.DMA(())],
    )
    return pl.pallas_call(
        functools.partial(_moe_kernel, tm=tm),
        grid_spec=grid_spec,
        out_shape=jax.ShapeDtypeStruct((nt * tm, d), F32),
        compiler_params=_params(("arbitrary", "arbitrary")),
        name="moe_ffn",
    )(tile_exp, tile_valid, row_tok, row_gate, h, g, wg, wu, wd)


def _route_plan(idx, gates, n_exp, tm):
    t = idx.shape[0]
    n_pairs = t * TOP_K
    nt = (n_pairs + n_exp * (tm - 1)) // tm
    e_flat = idx.reshape(n_pairs)
    onehot = (e_flat[:, None] == jnp.arange(n_exp, dtype=jnp.int32)[None, :]).astype(jnp.int32)
    csum = jnp.cumsum(onehot, axis=0)
    counts = csum[-1]
    rank = jnp.sum((csum - onehot) * onehot, axis=1)
    tiles_per = (counts + tm - 1) // tm
    tile_end = jnp.cumsum(tiles_per)
    tile_start = tile_end - tiles_per
    pos = tile_start[e_flat] * tm + rank
    row_tok = jnp.zeros((nt * tm,), jnp.int32).at[pos].set(
        jnp.arange(n_pairs, dtype=jnp.int32) // TOP_K)
    row_gate = jnp.zeros((nt * tm,), F32).at[pos].set(gates.reshape(n_pairs))
    tile_id = jnp.arange(nt, dtype=jnp.int32)
    tile_exp = jnp.sum((tile_id[:, None] >= tile_end[None, :]).astype(jnp.int32), axis=1)
    tile_valid = (tile_id < tile_end[-1]).astype(jnp.int32)
    last_exp = jnp.max(jnp.where(tiles_per > 0, jnp.arange(n_exp, dtype=jnp.int32), 0))
    tile_exp = jnp.where(tile_valid > 0, tile_exp, last_exp).astype(jnp.int32)
    return tile_exp, tile_valid, row_tok.reshape(nt, 1, tm), row_gate.reshape(nt * tm, 1), pos


def _combine_kernel(pos_ref, h_ref, y_hbm, o_ref, ybuf, sem, *, tc):
    def row_copy(r):
        return pltpu.make_async_copy(y_hbm.at[pl.ds(pos_ref[0, 0, r], 1)],
                                     ybuf.at[pl.ds(r, 1)], sem)

    def issue(r, c):
        row_copy(r).start()
        return c

    lax.fori_loop(0, TOP_K * tc, issue, 0)

    def drain(r, c):
        row_copy(r).wait()
        return c

    lax.fori_loop(0, TOP_K * tc, drain, 0)
    o_ref[...] = h_ref[...] + ybuf[0:tc, :] + ybuf[tc:, :]


def _combine(h, ys, pos, *, tc):
    t, d = h.shape
    nb = _cdiv(t, tc)
    p = jnp.zeros((nb * tc, TOP_K), jnp.int32).at[:t].set(pos.reshape(t, TOP_K))
    p = p.reshape(nb, tc, TOP_K).transpose(0, 2, 1).reshape(nb, 1, TOP_K * tc)
    return pl.pallas_call(
        functools.partial(_combine_kernel, tc=tc),
        grid=(nb,),
        in_specs=[
            pl.BlockSpec((1, 1, TOP_K * tc), lambda i: (i, 0, 0), memory_space=pltpu.SMEM),
            pl.BlockSpec((tc, d), lambda i: (i, 0)),
            pl.BlockSpec(memory_space=pl.ANY),
        ],
        out_specs=pl.BlockSpec((tc, d), lambda i: (i, 0)),
        out_shape=jax.ShapeDtypeStruct((t, d), F32),
        scratch_shapes=[pltpu.VMEM((TOP_K * tc, d), F32), pltpu.SemaphoreType.DMA(())],
        compiler_params=_params(("arbitrary",)),
        name="moe_combine",
    )(p, h, ys)


def _moe_layer(h, g, w_router_pad, wg, wu, wd):
    n_exp = wg.shape[0]
    t = h.shape[0]
    tm = 512 if t >= 4096 else 128
    tf = _pick(wg.shape[2], (512, 256, 128))
    idx, gates = _router(h, g, w_router_pad, n_exp)
    tile_exp, tile_valid, row_tok, row_gate, pos = _route_plan(idx, gates, n_exp, tm)
    ys = _moe_ffn(h, g, wg, wu, wd, tile_exp, tile_valid, row_tok, row_gate, tm=tm, tf=tf)
    return _combine(h, ys, pos, tc=min(256, t))


def _rope_cols(x, cos, sa, sb):
    outs = []
    for c in range(x.shape[1] // LANES):
        xc = x[:, c * LANES:(c + 1) * LANES]
        outs.append(xc * cos + pltpu.roll(xc, LANES - ROT_DIM // 2, 1) * sa
                    + pltpu.roll(xc, ROT_DIM // 2, 1) * sb)
    return outs


def _rope_tables(n_real_pos, rows):
    half = ROT_DIM // 2
    pos = jnp.concatenate([N_META + jnp.arange(n_real_pos), jnp.arange(rows - n_real_pos)])
    inv = jnp.power(jnp.float32(ROPE_THETA), -jnp.arange(half, dtype=F32) / half)
    ang = pos.astype(F32)[:, None] * inv[None, :]
    cos, sin = jnp.cos(ang), jnp.sin(ang)
    z = jnp.zeros((rows, HEAD_DIM - ROT_DIM), F32)
    zh = jnp.zeros((rows, half), F32)
    c_head = jnp.concatenate([cos, cos, z + 1.0], axis=1)
    a_head = jnp.concatenate([-sin, zh, z], axis=1)
    b_head = jnp.concatenate([zh, sin, z], axis=1)
    rep = LANES // HEAD_DIM
    return (jnp.tile(c_head, (1, rep)), jnp.tile(a_head, (1, rep)), jnp.tile(b_head, (1, rep)))


def _rope_block_map(bps, meta_blk):
    return lambda i: (jnp.where(i >= meta_blk, bps, i % bps), 0)


def _kv_kernel(h_ref, g_ref, w_ref, cos_ref, sa_ref, sb_ref, k_ref, v_ref):
    xn = _rms(h_ref[...], g_ref[...]).astype(BF16)
    kv = jnp.dot(xn, w_ref[...], preferred_element_type=F32)
    nk = k_ref.shape[1]
    ks = _rope_cols(kv[:, :nk], cos_ref[...], sa_ref[...], sb_ref[...])
    for c, kc in enumerate(ks):
        k_ref[:, c * LANES:(c + 1) * LANES] = kc.astype(BF16)
    v_ref[...] = kv[:, nk:].astype(BF16)


def _q_kernel(h_ref, g_ref, w_ref, cos_ref, sa_ref, sb_ref, q_ref):
    xn = _rms(h_ref[...], g_ref[...]).astype(BF16)
    q = jnp.dot(xn, w_ref[...], preferred_element_type=F32)
    qs = _rope_cols(q, cos_ref[...], sa_ref[...], sb_ref[...])
    scale = HEAD_DIM ** -0.5
    for c, qc in enumerate(qs):
        q_ref[:, c * LANES:(c + 1) * LANES] = (qc * scale).astype(BF16)


def _proj_rope(kernel, h, g, w, tables, outs, *, n_real, seq, name):
    t, d = h.shape
    tm = PAD_ROWS
    bps = seq // tm
    meta_blk = n_real // tm
    tmap = _rope_block_map(bps, meta_blk)
    tspec = pl.BlockSpec((tm, LANES), tmap)
    return pl.pallas_call(
        kernel,
        grid=(t // tm,),
        in_specs=[
            pl.BlockSpec((tm, d), lambda i: (i, 0)),
            pl.BlockSpec((1, d), lambda i: (0, 0)),
            pl.BlockSpec(w.shape, lambda i: (0, 0)),
            tspec, tspec, tspec,
        ],
        out_specs=[pl.BlockSpec((tm, n), lambda i: (i, 0)) for n in outs],
        out_shape=[jax.ShapeDtypeStruct((t, n), BF16) for n in outs],
        compiler_params=_params(("parallel",)),
        name=name,
    )(h, g, w, *tables)


def _attn_kernel(sink_ref, q_ref, km_ref, kp_ref, kc_ref, vm_ref, vp_ref, vc_ref, o_ref,
                 *, meta_blk, bps, n_kv):
    i = pl.program_id(0)
    is_meta = i == meta_blk
    first = (i % bps) == 0
    tq = q_ref.shape[0]
    nm = km_ref.shape[0]
    r = lax.broadcasted_iota(jnp.int32, (tq, nm + 2 * tq), 0)
    c = lax.broadcasted_iota(jnp.int32, (tq, nm + 2 * tq), 1)
    see_meta = jnp.logical_and(c < nm, jnp.logical_not(is_meta))
    jp = c - nm
    see_prev = jnp.logical_and(jnp.logical_and(jp >= 0, jp < tq),
                               jnp.logical_and(jp > r + (tq - WINDOW),
                                               jnp.logical_not(jnp.logical_or(is_meta, first))))
    jc = c - nm - tq
    see_cur = jnp.logical_and(jc >= 0, jnp.logical_and(jc <= r, r - jc < WINDOW))
    mask = jnp.logical_or(see_meta, jnp.logical_or(see_prev, see_cur))
    neg = jnp.float32(-jnp.inf)
    for hk in range(n_kv):
        ks = slice(hk * HEAD_DIM, (hk + 1) * HEAD_DIM)
        kcat = jnp.concatenate([km_ref[:, ks], kp_ref[:, ks], kc_ref[:, ks]], axis=0)
        vcat = jnp.concatenate([vm_ref[:, ks], vp_ref[:, ks], vc_ref[:, ks]], axis=0)
        for gq in range(Q_PER_KV):
            hq = hk * Q_PER_KV + gq
            qs = slice(hq * HEAD_DIM, (hq + 1) * HEAD_DIM)
            s = lax.dot_general(q_ref[:, qs], kcat, (((1,), (1,)), ((), ())),
                                preferred_element_type=F32)
            s = jnp.where(mask, s, neg)
            sink = sink_ref[hq]
            m = jnp.maximum(jnp.max(s, axis=-1, keepdims=True), sink)
            p = jnp.exp(s - m)
            den = jnp.sum(p, axis=-1, keepdims=True) + jnp.exp(sink - m)
            o = jnp.dot(p.astype(BF16), vcat, preferred_element_type=F32)
            o_ref[:, qs] = (o / den).astype(BF16)


def _attention(q, k, v, sinks, *, n_real, seq):
    t, d = q.shape
    dk = k.shape[1]
    tq = PAD_ROWS
    bps = seq // tq
    meta_blk = n_real // tq
    meta16 = n_real // N_META
    kv_meta = pl.BlockSpec((N_META, dk), lambda i, s: (meta16, 0))
    kv_prev = pl.BlockSpec((tq, dk), lambda i, s: (jnp.maximum(i - 1, 0), 0))
    kv_cur = pl.BlockSpec((tq, dk), lambda i, s: (i, 0))
    grid_spec = pltpu.PrefetchScalarGridSpec(
        num_scalar_prefetch=1,
        grid=(t // tq,),
        in_specs=[pl.BlockSpec((tq, d), lambda i, s: (i, 0)),
                  kv_meta, kv_prev, kv_cur, kv_meta, kv_prev, kv_cur],
        out_specs=pl.BlockSpec((tq, d), lambda i, s: (i, 0)),
    )
    return pl.pallas_call(
        functools.partial(_attn_kernel, meta_blk=meta_blk, bps=bps, n_kv=dk // HEAD_DIM),
        grid_spec=grid_spec,
        out_shape=jax.ShapeDtypeStruct((t, d), BF16),
        compiler_params=_params(("parallel",)),
        name="swa_attention",
    )(sinks, q, k, k, k, v, v, v)


def _oproj_kernel(h_ref, o_ref, w_ref, out_ref):
    out_ref[...] = h_ref[...] + jnp.dot(o_ref[...], w_ref[...], preferred_element_type=F32)


def _oproj(h, o, w):
    t, d = h.shape
    tm = min(512, t)
    return pl.pallas_call(
        _oproj_kernel,
        grid=(_cdiv(t, tm),),
        in_specs=[pl.BlockSpec((tm, d), lambda i: (i, 0)),
                  pl.BlockSpec((tm, o.shape[1]), lambda i: (i, 0)),
                  pl.BlockSpec(w.shape, lambda i: (0, 0))],
        out_specs=pl.BlockSpec((tm, d), lambda i: (i, 0)),
        out_shape=jax.ShapeDtypeStruct((t, d), F32),
        compiler_params=_params(("parallel",)),
        name="attn_out_proj",
    )(h, o, w)


def _final_kernel(h_ref, g_ref, o_ref):
    o_ref[...] = _rms(h_ref[...], g_ref[...])


def _final_norm(h, g, n_real):
    d = h.shape[1]
    tm = min(512, n_real)
    return pl.pallas_call(
        _final_kernel,
        grid=(n_real // tm,),
        in_specs=[pl.BlockSpec((tm, d), lambda i: (i, 0)),
                  pl.BlockSpec((1, d), lambda i: (0, 0))],
        out_specs=pl.BlockSpec((tm, d), lambda i: (i, 0)),
        out_shape=jax.ShapeDtypeStruct((n_real, d), F32),
        compiler_params=_params(("parallel",)),
        name="final_norm",
    )(h, g)


def kernel(x, meta_tokens, norm_mix, norm_ffn, norm_kv, norm_final, pool_w, pool_scale, w_kv, w_q,
           sinks, w_o, dense_w_gate, dense_w_up, dense_w_down, router_w, moe_w_gate, moe_w_up,
           moe_w_down):
    b, s, d = x.shape
    depth = norm_mix.shape[0]
    n_a = pool_w.shape[0]
    n_exp = router_w.shape[2]
    n_real = b * s
    assert meta_tokens.shape[0] == N_META and s % PAD_ROWS == 0 and d % LANES == 0
    assert w_kv.shape[1] // (2 * HEAD_DIM) * Q_PER_KV * HEAD_DIM == w_q.shape[2]

    h = jnp.concatenate([x.reshape(n_real, d), meta_tokens.astype(x.dtype),
                         jnp.zeros((PAD_ROWS - N_META, d), x.dtype)], axis=0)
    tables = _rope_tables(s, s + PAD_ROWS)
    router_pad = jnp.zeros((router_w.shape[0], d, LANES), F32).at[:, :, :n_exp].set(router_w)
    row = lambda a: a.reshape(1, -1)

    k_shared = v_shared = None
    for layer in range(depth):
        if layer < n_a:
            h = _pool_layer(h, row(norm_mix[layer]), pool_w[layer].astype(BF16),
                            row(pool_scale[layer]), n_real=n_real, seq=s)
        else:
            a = layer - n_a
            (q,) = _proj_rope(_q_kernel, h, row(norm_mix[layer]), w_q[a].astype(BF16), tables,
                              (w_q.shape[2],), n_real=n_real, seq=s, name="q_proj")
            o = _attention(q, k_shared, v_shared, sinks[a], n_real=n_real, seq=s)
            h = _oproj(h, o, w_o[a].astype(BF16))
        j = layer // 2
        if layer % 2 == 0:
            h = _ffn_dense(h, row(norm_ffn[layer]), dense_w_gate[j].astype(BF16),
                           dense_w_up[j].astype(BF16), dense_w_down[j].astype(BF16))
        else:
            h = _moe_layer(h, row(norm_ffn[layer]), router_pad[j], moe_w_gate[j].astype(BF16),
                           moe_w_up[j].astype(BF16), moe_w_down[j].astype(BF16))
        if layer == n_a - 1:
            nk = w_kv.shape[1] // 2
            k_shared, v_shared = _proj_rope(_kv_kernel, h, row(norm_kv), w_kv.astype(BF16), tables,
                                            (nk, nk), n_real=n_real, seq=s, name="kv_proj")
    return _final_norm(h, row(norm_final), n_real).reshape(b, s, d)
```
